```python
import math
import jax, jax.numpy as jnp
from jax import lax
import numpy as np

D_MODEL = 1024
BATCH = 2
SEQ = 8192
DEPTH = 1
DEC_BATCH = 128
DEC_SEQ = 4
PAST_LEN = 8192
PAGE_SIZE = 128

N_META = 16
MLA_HEADS = 8
MLA_Q_RANK = 384
MLA_KV_RANK = 256
MLA_NOPE = 64
MLA_ROPE = 32
MLA_V = 64
ROPE_THETA = 10000.0
MLA_SCALE = 1.0 / math.sqrt(MLA_NOPE + MLA_ROPE)
DIFF_HEADS = 4
DIFF_QK = 64
DIFF_V = 2 * DIFF_QK
DIFF_SCALE = 1.0 / math.sqrt(DIFF_QK)
T5_BUCKETS = 32
T5_MAX_DIST = 128
D_FF = -(-8 * D_MODEL // (3 * 256)) * 256
Q_BLOCK = 128
EPS = 1e-6
OFF_CKV = MLA_Q_RANK
OFF_KPE = OFF_CKV + MLA_KV_RANK
OFF_QB = OFF_KPE + MLA_ROPE
OFF_KB = OFF_QB + DIFF_HEADS * 2 * DIFF_QK
OFF_VB = OFF_KB + DIFF_HEADS * 2 * DIFF_QK
OFF_GATE = OFF_VB + DIFF_HEADS * DIFF_V
IN_WIDTH = OFF_GATE + 2 * D_MODEL
IN_SPLITS = (OFF_CKV, OFF_KPE, OFF_QB, OFF_KB, OFF_VB, OFF_GATE)

kernel_name = "mla_diffattn_gated_hybrid_step"

F32 = jnp.float32


def rms_norm(x, g):
    xf = x.astype(F32)
    y = xf * lax.rsqrt(jnp.mean(xf * xf, axis=-1, keepdims=True) + EPS)
    return (y * g.astype(F32)).astype(x.dtype)


def rope(x, pos):
    half = x.shape[-1] // 2
    inv = ROPE_THETA ** (-jnp.arange(half, dtype=F32) / half)
    ang = pos.astype(F32)[:, None] * inv[None, :]
    cos = jnp.cos(ang)[None, :, None, :]
    sin = jnp.sin(ang)[None, :, None, :]
    xf = x.astype(F32)
    x1, x2 = xf[..., :half], xf[..., half:]
    return jnp.concatenate([x1 * cos - x2 * sin, x1 * sin + x2 * cos], axis=-1).astype(x.dtype)


def t5_bucket(qpos, kpos):
    n = jnp.maximum(qpos[:, None] - kpos[None, :], 0)
    exact = T5_BUCKETS // 2
    nf = jnp.maximum(n, 1).astype(F32)
    large = exact + (jnp.log(nf / exact) / math.log(T5_MAX_DIST / exact) * (T5_BUCKETS - exact)).astype(jnp.int32)
    return jnp.where(n < exact, n, jnp.minimum(large, T5_BUCKETS - 1))


def t5_bias(qpos, kpos, table):
    return jnp.transpose(table[t5_bucket(qpos, kpos)], (2, 0, 1)).astype(F32)


def masked_softmax(s, mask):
    return jax.nn.softmax(jnp.where(mask, s, -jnp.inf), axis=-1)


def token_features(h, pos, lw):
    B, T, _ = h.shape
    z = h @ lw["w_in"]
    cq, ckv, kpe, qb, kb, vb, gates = jnp.split(z, IN_SPLITS, axis=-1)
    q = (rms_norm(cq, lw["g_cq"]) @ lw["w_uq"]).reshape(B, T, MLA_HEADS, MLA_NOPE + MLA_ROPE)
    q_nope = rms_norm(q[..., :MLA_NOPE], lw["g_qn"])
    q_rope = rope(rms_norm(q[..., MLA_NOPE:], lw["g_qr"]), pos)
    ckv = rms_norm(ckv, lw["g_ckv"])
    kpe = rope(rms_norm(kpe, lw["g_kr"])[:, :, None, :], pos)[:, :, 0, :]
    qb = rms_norm(qb.reshape(B, T, DIFF_HEADS, 2, DIFF_QK), lw["g_qb"]).reshape(B, T, DIFF_HEADS, 2 * DIFF_QK)
    kb = rms_norm(kb.reshape(B, T, DIFF_HEADS, 2, DIFF_QK), lw["g_kb"]).reshape(B, T, DIFF_HEADS, 2 * DIFF_QK)
    vb = vb.reshape(B, T, DIFF_HEADS, DIFF_V)
    ga, gb = jnp.split(jax.nn.sigmoid(gates.astype(F32)).astype(h.dtype), 2, axis=-1)
    return q_nope, q_rope, ckv, kpe, qb, kb, vb, ga, gb


def expand_latent(ckv, lw):
    B, Tk, _ = ckv.shape
    kv = (ckv @ lw["w_ukv"]).reshape(B, Tk, MLA_HEADS, MLA_NOPE + MLA_V)
    return rms_norm(kv[..., :MLA_NOPE], lw["g_kn"]), kv[..., MLA_NOPE:]


def mla_attend(q_nope, q_rope, k_nope, kpe, v, qpos, kpos):
    B, Tq = q_nope.shape[:2]
    s = (jnp.einsum("bqhd,bkhd->bhqk", q_nope, k_nope, preferred_element_type=F32)
         + jnp.einsum("bqhr,bkr->bhqk", q_rope, kpe, preferred_element_type=F32)) * MLA_SCALE
    p = masked_softmax(s, kpos[None, :] <= qpos[:, None])
    o = jnp.einsum("bhqk,bkhd->bqhd", p, v.astype(F32))
    return o.reshape(B, Tq, MLA_HEADS * MLA_V).astype(v.dtype)


def diff_attend(qb, kb, vb, qpos, kpos, lam, lam_init, bias_table, g_sub):
    B, Tq = qb.shape[:2]
    bias = t5_bias(qpos, kpos, bias_table)[None]
    mask = kpos[None, :] <= qpos[:, None]
    s1 = jnp.einsum("bqhd,bkhd->bhqk", qb[..., :DIFF_QK], kb[..., :DIFF_QK], preferred_element_type=F32) * DIFF_SCALE + bias
    s2 = jnp.einsum("bqhd,bkhd->bhqk", qb[..., DIFF_QK:], kb[..., DIFF_QK:], preferred_element_type=F32) * DIFF_SCALE + bias
    a = masked_softmax(s1, mask) - lam * masked_softmax(s2, mask)
    o = jnp.einsum("bhqk,bkhd->bqhd", a, vb.astype(F32))
    o = rms_norm(o, g_sub) * (1.0 - lam_init)
    return o.reshape(B, Tq, DIFF_HEADS * DIFF_V).astype(vb.dtype)


def merge_and_ffn(x, out_a, out_b, ga, gb, lw):
    m = ga * (out_a @ lw["w_oa"]) + gb * (out_b @ lw["w_ob"])
    x = x + m @ lw["w_o"]
    gu = rms_norm(x, lw["g_ffn"]) @ lw["w_gu"]
    g, u = jnp.split(gu, 2, axis=-1)
    return x + (jax.nn.silu(g) * u) @ lw["w_down"]


def prompt_layer(x, lw, lam, lam_init, bias_table):
    B, T, _ = x.shape
    pos = jnp.arange(T)
    h = rms_norm(x, lw["g_attn"])
    q_nope, q_rope, ckv, kpe, qb, kb, vb, ga, gb = token_features(h, pos, lw)
    k_nope, v_a = expand_latent(ckv, lw)
    n_blk = -(-T // Q_BLOCK)
    Tp = n_blk * Q_BLOCK

    def padt(a):
        return jnp.pad(a, [(0, 0), (0, Tp - T)] + [(0, 0)] * (a.ndim - 2))

    qn_p, qr_p, kn_p, kpe_p, va_p = padt(q_nope), padt(q_rope), padt(k_nope), padt(kpe), padt(v_a)
    qb_p, kb_p, vb_p = padt(qb), padt(kb), padt(vb)
    kpos = jnp.arange(Tp)

    def block(i):
        s = i * Q_BLOCK
        qpos = s + jnp.arange(Q_BLOCK)
        sl = lambda a: lax.dynamic_slice_in_dim(a, s, Q_BLOCK, axis=1)
        oa = mla_attend(sl(qn_p), sl(qr_p), kn_p, kpe_p, va_p, qpos, kpos)
        ob = diff_attend(sl(qb_p), kb_p, vb_p, qpos, kpos, lam, lam_init, bias_table, lw["g_sub"])
        return oa, ob

    oa, ob = lax.map(block, jnp.arange(n_blk))
    unblock = lambda o: jnp.moveaxis(o, 0, 1).reshape(B, Tp, o.shape[-1])[:, :T]
    x = merge_and_ffn(x, unblock(oa), unblock(ob), ga, gb, lw)
    return x, (ckv, kpe, kb, vb)


def sample_layer(x, c_ckv, c_kpe, c_kb, c_vb, page_table, lw, lam, lam_init, bias_table):
    _, Tn, _ = x.shape
    past_len = page_table.shape[1] * c_ckv.shape[1]
    qpos = past_len + jnp.arange(Tn)
    kpos = jnp.arange(past_len + Tn)
    h = rms_norm(x, lw["g_attn"])
    q_nope, q_rope, ckv, kpe, qb, kb, vb, ga, gb = token_features(h, qpos, lw)

    def per_seq(args):
        pt, qn, qr, ckv_n, kpe_n, qb_n, kb_n, vb_n = args
        gather = lambda c, new: jnp.concatenate([c[pt].reshape((-1,) + c.shape[2:]), new], axis=0)[None]
        ckv_all, kpe_all = gather(c_ckv, ckv_n), gather(c_kpe, kpe_n)
        kb_all, vb_all = gather(c_kb, kb_n), gather(c_vb, vb_n)
        k_nope, v_a = expand_latent(ckv_all, lw)
        oa = mla_attend(qn[None], qr[None], k_nope, kpe_all, v_a, qpos, kpos)[0]
        ob = diff_attend(qb_n[None], kb_all, vb_all, qpos, kpos, lam, lam_init, bias_table, lw["g_sub"])[0]
        return oa, ob

    oa, ob = lax.map(per_seq, (page_table, q_nope, q_rope, ckv, kpe, qb, kb, vb))
    x = merge_and_ffn(x, oa, ob, ga, gb, lw)
    return x, (ckv, kpe, kb, vb)


def setup_inputs(seed: int = 0) -> dict:
    key = jax.random.key(seed)
    keys = jax.random.split(key, 40)
    ctr = [0]

    def nk():
        ctr[0] += 1
        return keys[ctr[0] - 1]

    def nrm(shape, scale):
        return jax.random.normal(nk(), shape, F32) * scale

    def gain(shape):
        return 1.0 + 0.02 * jax.random.normal(nk(), shape, F32)

    n_pages = PAST_LEN // PAGE_SIZE
    n_used = DEC_BATCH * n_pages
    n_pool = n_used + max(1, n_used // 4)
    L = DEPTH
    x_prompt = nrm((BATCH, SEQ, D_MODEL), 1.0)
    x_sample = nrm((DEC_BATCH, DEC_SEQ, D_MODEL), 1.0)
    cache_ckv = nrm((L, n_pool, PAGE_SIZE, MLA_KV_RANK), 1.0)
    cache_kpe = nrm((L, n_pool, PAGE_SIZE, MLA_ROPE), 1.0)
    cache_kb = nrm((L, n_pool, PAGE_SIZE, DIFF_HEADS, 2 * DIFF_QK), 1.0)
    cache_vb = nrm((L, n_pool, PAGE_SIZE, DIFF_HEADS, DIFF_V), 1.0)
    page_table = jax.random.permutation(nk(), n_pool)[:n_used].reshape(DEC_BATCH, n_pages).astype(jnp.int32)
    return {
        "x_prompt": x_prompt,
        "x_sample": x_sample,
        "cache_ckv": cache_ckv,
        "cache_kpe": cache_kpe,
        "cache_kb": cache_kb,
        "cache_vb": cache_vb,
        "page_table": page_table,
        "meta_tokens": nrm((N_META, D_MODEL), 1.0),
        "rel_bias": nrm((T5_BUCKETS, DIFF_HEADS), 0.5),
        "g_attn": gain((L, D_MODEL)),
        "w_in": nrm((L, D_MODEL, IN_WIDTH), D_MODEL ** -0.5),
        "g_cq": gain((L, MLA_Q_RANK)),
        "w_uq": nrm((L, MLA_Q_RANK, MLA_HEADS * (MLA_NOPE + MLA_ROPE)), MLA_Q_RANK ** -0.5),
        "g_qn": gain((L, MLA_NOPE)),
        "g_qr": gain((L, MLA_ROPE)),
        "g_ckv": gain((L, MLA_KV_RANK)),
        "g_kr": gain((L, MLA_ROPE)),
        "w_ukv": nrm((L, MLA_KV_RANK, MLA_HEADS * (MLA_NOPE + MLA_V)), MLA_KV_RANK ** -0.5),
        "g_kn": gain((L, MLA_NOPE)),
        "g_qb": gain((L, 2, DIFF_QK)),
        "g_kb": gain((L, 2, DIFF_QK)),
        "lam_q1": nrm((L, DIFF_QK), 0.1),
        "lam_k1": nrm((L, DIFF_QK), 0.1),
        "lam_q2": nrm((L, DIFF_QK), 0.1),
        "lam_k2": nrm((L, DIFF_QK), 0.1),
        "g_sub": gain((L, DIFF_V)),
        "w_oa": nrm((L, MLA_HEADS * MLA_V, D_MODEL), (MLA_HEADS * MLA_V) ** -0.5),
        "w_ob": nrm((L, DIFF_HEADS * DIFF_V, D_MODEL), (DIFF_HEADS * DIFF_V) ** -0.5),
        "w_o": nrm((L, D_MODEL, D_MODEL), D_MODEL ** -0.5),
        "g_ffn": gain((L, D_MODEL)),
        "w_gu": nrm((L, D_MODEL, 2 * D_FF), D_MODEL ** -0.5),
        "w_down": nrm((L, D_FF, D_MODEL), D_FF ** -0.5),
    }


def reference(x_prompt, x_sample, cache_ckv, cache_kpe, cache_kb, cache_vb, page_table,
              meta_tokens, rel_bias, g_attn, w_in, g_cq, w_uq, g_qn, g_qr, g_ckv, g_kr,
              w_ukv, g_kn, g_qb, g_kb, lam_q1, lam_k1, lam_q2, lam_k2, g_sub,
              w_oa, w_ob, w_o, g_ffn, w_gu, w_down):
    B = x_prompt.shape[0]
    meta = jnp.broadcast_to(meta_tokens[None].astype(x_prompt.dtype), (B, N_META, x_prompt.shape[-1]))
    xp = jnp.concatenate([meta, x_prompt], axis=1)
    xs = x_sample
    rows_p, rows_s = [], []
    for l in range(DEPTH):
        lw = dict(g_attn=g_attn[l], w_in=w_in[l], g_cq=g_cq[l], w_uq=w_uq[l], g_qn=g_qn[l],
                  g_qr=g_qr[l], g_ckv=g_ckv[l], g_kr=g_kr[l], w_ukv=w_ukv[l], g_kn=g_kn[l],
                  g_qb=g_qb[l], g_kb=g_kb[l], g_sub=g_sub[l], w_oa=w_oa[l], w_ob=w_ob[l],
                  w_o=w_o[l], g_ffn=g_ffn[l], w_gu=w_gu[l], w_down=w_down[l])
        lam_init = 0.8 - 0.6 * math.exp(-0.3 * l)
        lam = (jnp.exp(jnp.sum(lam_q1[l].astype(F32) * lam_k1[l].astype(F32)))
               - jnp.exp(jnp.sum(lam_q2[l].astype(F32) * lam_k2[l].astype(F32))) + lam_init)
        xp, rp = prompt_layer(xp, lw, lam, lam_init, rel_bias)
        xs, rs = sample_layer(xs, cache_ckv[l], cache_kpe[l], cache_kb[l], cache_vb[l],
                              page_table, lw, lam, lam_init, rel_bias)
        rows_p.append(rp)
        rows_s.append(rs)
    y_prompt = xp[:, N_META:]
    y_sample = xs
    ckv_p = jnp.stack([r[0] for r in rows_p])
    kpe_p = jnp.stack([r[1] for r in rows_p])
    kb_p = jnp.stack([r[2] for r in rows_p])
    vb_p = jnp.stack([r[3] for r in rows_p])
    ckv_s = jnp.stack([r[0] for r in rows_s])
    kpe_s = jnp.stack([r[1] for r in rows_s])
    kb_s = jnp.stack([r[2] for r in rows_s])
    vb_s = jnp.stack([r[3] for r in rows_s])
    return (y_prompt, y_sample, ckv_p, kpe_p, kb_p, vb_p, ckv_s, kpe_s, kb_s, vb_s)
```

```python
import functools
import math

import numpy as np
import jax
import jax.numpy as jnp
from jax import lax
from jax.experimental import pallas as pl
from jax.experimental.pallas import tpu as pltpu

F32 = jnp.float32
BF16 = jnp.bfloat16

N_META = 16
MLA_HEADS = 8
MLA_Q_RANK = 384
MLA_KV_RANK = 256
MLA_NOPE = 64
MLA_ROPE = 32
MLA_V = 64
ROPE_THETA = 10000.0
MLA_SCALE = 1.0 / math.sqrt(MLA_NOPE + MLA_ROPE)
DIFF_HEADS = 4
DIFF_QK = 64
DIFF_V = 2 * DIFF_QK
DIFF_SCALE = 1.0 / math.sqrt(DIFF_QK)
T5_BUCKETS = 32
T5_MAX_DIST = 128
EPS = 1e-6
LAM_INIT = 0.8 - 0.6 * math.exp(-0.3 * 0)

LANES = 128
MXU_DIM = 256
VMEM_LIMIT_BYTES = 56 * 1024 * 1024

ROW_TILE = 256
ATT_BLOCK = 256
PAGES_PER_STEP = 8
NEG = -1e30

N_UNITS = 16
HEAD_PAD = 128

ZC_CQ = 0
ZC_CKV = ZC_CQ + MLA_Q_RANK
ZC_KPE = ZC_CKV + MLA_KV_RANK
ZC_QB = ZC_KPE + HEAD_PAD
ZC_KB = ZC_QB + DIFF_HEADS * 2 * DIFF_QK
ZC_VB = ZC_KB + DIFF_HEADS * 2 * DIFF_QK
ZC_GATE = ZC_VB + DIFF_HEADS * DIFF_V


def _cparams(sem):
    return pltpu.CompilerParams(dimension_semantics=sem, vmem_limit_bytes=VMEM_LIMIT_BYTES)


def _const_spec(shape):
    nd = len(shape)
    return pl.BlockSpec(shape, lambda *_: (0,) * nd)


def _rms(x, n):
    ss = jnp.sum(x * x, axis=-1, keepdims=True)
    return x * lax.rsqrt(ss * (1.0 / n) + EPS)


def _seg_rms(x, seg_mat):
    outs = []
    for c in range(x.shape[1] // MXU_DIM):
        xc = x[:, c * MXU_DIM:(c + 1) * MXU_DIM]
        ms = jnp.dot((xc * xc).astype(BF16), seg_mat, preferred_element_type=F32)
        outs.append(xc * lax.rsqrt(ms + EPS))
    return jnp.concatenate(outs, axis=1)


def _rope(x, c, s_up, s_dn):
    w = x.shape[1]
    return x * c + pltpu.roll(x, w - MLA_ROPE // 2, 1) * s_up + pltpu.roll(x, MLA_ROPE // 2, 1) * s_dn


def _feature_kernel(x_ref, tab_ref, gattn_ref, win_ref, gcq_ref, wuq_ref, gq_ref, gckv_ref, wkv_ref,
                    gkn_ref, gkpe_ref, gqb_ref, gkb_ref, sq_ref, sb_ref,
                    ckv_ref, kpe_ref, kb_ref, vb_ref, gate_ref, q_ref, k_ref, v_ref):
    x = x_ref[...]
    h = _rms(x, x.shape[1]) * gattn_ref[...]
    z = jnp.dot(h.astype(BF16), win_ref[...], preferred_element_type=F32)

    tab = tab_ref[...]
    rc, rs_up, rs_dn = tab[:, 0:LANES], tab[:, LANES:2 * LANES], tab[:, 2 * LANES:3 * LANES]
    tile8 = lambda a: jnp.concatenate([a] * MLA_HEADS, axis=1)

    gate_ref[...] = jax.nn.sigmoid(z[:, ZC_GATE:])

    ckv = _rms(z[:, ZC_CKV:ZC_KPE], MLA_KV_RANK) * gckv_ref[...]
    ckv_ref[...] = ckv
    kv = jnp.dot(ckv.astype(BF16), wkv_ref[...], preferred_element_type=F32)
    kn = _seg_rms(kv[:, :MLA_HEADS * HEAD_PAD], sq_ref[...]) * gkn_ref[...]

    kp = z[:, ZC_KPE:ZC_QB]
    kp = _rms(kp, MLA_ROPE) * gkpe_ref[...]
    kp = _rope(kp, rc, rs_up, rs_dn)
    kpe_ref[...] = kp
    k_ref[:, :MLA_HEADS * HEAD_PAD] = (kn + tile8(kp)).astype(k_ref.dtype)
    v_ref[:, :MLA_HEADS * MLA_V] = kv[:, MLA_HEADS * HEAD_PAD:].astype(v_ref.dtype)

    cq = _rms(z[:, ZC_CQ:ZC_CKV], MLA_Q_RANK) * gcq_ref[...]
    q = jnp.dot(cq.astype(BF16), wuq_ref[...], preferred_element_type=F32)
    q = _seg_rms(q, sq_ref[...]) * gq_ref[...]
    q = _rope(q, tile8(rc), tile8(rs_up), tile8(rs_dn))
    q_ref[:, :MLA_HEADS * HEAD_PAD] = q.astype(q_ref.dtype)

    qb = _seg_rms(z[:, ZC_QB:ZC_KB], sb_ref[...]) * gqb_ref[...]
    lane = lax.broadcasted_iota(jnp.int32, (x.shape[0], HEAD_PAD), 1)
    first = lane < DIFF_QK
    parts = []
    for hh in range(DIFF_HEADS):
        qh = qb[:, hh * HEAD_PAD:(hh + 1) * HEAD_PAD]
        parts.append(jnp.where(first, qh, 0.0))
        parts.append(jnp.where(first, 0.0, qh))
    q_ref[:, MLA_HEADS * HEAD_PAD:] = jnp.concatenate(parts, axis=1).astype(q_ref.dtype)

    kb = _seg_rms(z[:, ZC_KB:ZC_VB], sb_ref[...]) * gkb_ref[...]
    kb_ref[...] = kb
    k_ref[:, MLA_HEADS * HEAD_PAD:] = kb.astype(k_ref.dtype)
    vb = z[:, ZC_VB:ZC_GATE]
    vb_ref[...] = vb
    v_ref[:, MLA_HEADS * MLA_V:] = vb.astype(v_ref.dtype)


def _features(x3, tab, fw, q_dtype):
    nb, t, d = x3.shape
    assert t % ROW_TILE == 0
    grid = (nb, t // ROW_TILE)
    row = lambda w: pl.BlockSpec((None, ROW_TILE, w), lambda b, i: (b, i, 0))
    consts = [fw["g_attn"], fw["w_in"], fw["g_cq"], fw["w_uq"], fw["g_q"], fw["g_ckv"], fw["w_kv"],
              fw["g_kn"], fw["g_kpe"], fw["g_qb"], fw["g_kb"], fw["seg_q"], fw["seg_b"]]
    in_specs = [row(d), pl.BlockSpec((ROW_TILE, 3 * LANES), lambda b, i: (i, 0))]
    in_specs += [_const_spec(c.shape) for c in consts]
    widths = [(MLA_KV_RANK, F32), (HEAD_PAD, F32), (DIFF_HEADS * 2 * DIFF_QK, F32),
              (DIFF_HEADS * DIFF_V, F32), (2 * d, F32), (N_UNITS * HEAD_PAD, q_dtype),
              ((MLA_HEADS + DIFF_HEADS) * HEAD_PAD, BF16), (MLA_HEADS * MLA_V + DIFF_HEADS * DIFF_V, BF16)]
    out_shape = [jax.ShapeDtypeStruct((nb, t, w), dt) for w, dt in widths]
    out_specs = [row(w) for w, _ in widths]
    return pl.pallas_call(
        _feature_kernel, grid=grid, in_specs=in_specs, out_specs=out_specs, out_shape=out_shape,
        compiler_params=_cparams(("parallel", "parallel")), name="features",
    )(x3, tab, *consts)


def _prompt_attn_kernel(q_ref, k_ref, v_ref, bias_ref, o_ref):
    i = pl.program_id(2)
    q = q_ref[...]
    tq = q.shape[0]

    def step(j, carry, bias):
        m, l, acc = carry
        start = pl.multiple_of(j * ATT_BLOCK, ATT_BLOCK)
        k = k_ref[pl.ds(start, ATT_BLOCK), :]
        v = v_ref[pl.ds(start, ATT_BLOCK), :]
        s = lax.dot_general(q, k, (((1,), (1,)), ((), ())), preferred_element_type=F32)
        if bias is not None:
            s = s + bias
        m_new = jnp.maximum(m, jnp.max(s, axis=1, keepdims=True))
        alpha = jnp.exp(m - m_new)
        p = jnp.exp(s - m_new)
        l = alpha * l + jnp.sum(p, axis=1, keepdims=True)
        acc = alpha * acc + jnp.dot(p.astype(BF16), v, preferred_element_type=F32)
        return m_new, l, acc

    carry = (jnp.full((tq, 1), NEG, F32), jnp.zeros((tq, 1), F32), jnp.zeros((tq, HEAD_PAD), F32))
    carry = lax.fori_loop(0, jnp.maximum(i - 1, 0), lambda j, c: step(j, c, None), carry)
    carry = lax.cond(i >= 1, lambda c: step(i - 1, c, bias_ref[1]), lambda c: c, carry)
    m, l, acc = step(i, carry, bias_ref[0])
    o_ref[...] = acc / l


def _prompt_attention(q_all, k_all, v_all, bias):
    nb, t, _ = q_all.shape
    nq = t // ATT_BLOCK
    k_idx = lambda u: jnp.where(u < MLA_HEADS, u, MLA_HEADS + (u - MLA_HEADS) // 2)
    v_idx = lambda u: jnp.where(u < MLA_HEADS, u // 2, MLA_HEADS // 2 + (u - MLA_HEADS) // 2)
    b_idx = lambda u: jnp.where(u < MLA_HEADS, 0, 1 + (u - MLA_HEADS) // 2)
    in_specs = [
        pl.BlockSpec((None, ATT_BLOCK, HEAD_PAD), lambda b, u, i: (b, i, u)),
        pl.BlockSpec((None, t, HEAD_PAD), lambda b, u, i: (b, 0, k_idx(u))),
        pl.BlockSpec((None, t, HEAD_PAD), lambda b, u, i: (b, 0, v_idx(u))),
        pl.BlockSpec((None, 2, ATT_BLOCK, ATT_BLOCK), lambda b, u, i: (b_idx(u), 0, 0, 0)),
    ]
    return pl.pallas_call(
        _prompt_attn_kernel, grid=(nb, N_UNITS, nq), in_specs=in_specs,
        out_specs=pl.BlockSpec((None, ATT_BLOCK, HEAD_PAD), lambda b, u, i: (b, i, u)),
        out_shape=jax.ShapeDtypeStruct((nb, t, N_UNITS * HEAD_PAD), F32),
        compiler_params=_cparams(("parallel", "arbitrary", "arbitrary")), name="prompt_attn",
    )(q_all, k_all, v_all, bias)


S_ROWS_MLA = 64
S_ROWS_DIFF = 64
S_ROWS = S_ROWS_MLA + S_ROWS_DIFF


def _sample_attn_kernel(pt_ref, *refs):
    P = PAGES_PER_STEP
    ckv_pages = refs[0:P]
    kpe_pages = refs[P:2 * P]
    kb_pages = refs[2 * P:3 * P]
    vb_pages = refs[3 * P:4 * P]
    (qg_ref, ones_ref, qr_ref, qd_ref, wuk_ref, wuv_ref, ckvt_ref, kpet_ref, kbt_ref, vbt_ref,
     biasl_ref, biast_ref, o_ref, m_ref, l_ref, accm_ref, accd_ref) = refs[4 * P:]
    j = pl.program_id(1)
    last = pl.num_programs(1) - 1

    @pl.when(j == 0)
    def _():
        m_ref[...] = jnp.full(m_ref.shape, NEG, F32)
        l_ref[...] = jnp.zeros(l_ref.shape, F32)
        accm_ref[...] = jnp.zeros(accm_ref.shape, F32)
        accd_ref[...] = jnp.zeros(accd_ref.shape, F32)

    nt = (((1,), (1,)), ((), ()))

    def process(ckv, kpe, kb, vb, bias):
        ckv_b = ckv.astype(BF16)
        kexp = jnp.dot(ckv_b, wuk_ref[...], preferred_element_type=F32)
        s_raw = lax.dot_general(qg_ref[...], kexp.astype(BF16), nt, preferred_element_type=F32)
        ms = lax.dot_general(ones_ref[...], (kexp * kexp).astype(BF16), nt, preferred_element_type=F32)
        s_m = s_raw * lax.rsqrt(ms + EPS)
        s_m = s_m + lax.dot_general(qr_ref[...], kpe.astype(BF16), nt, preferred_element_type=F32)
        s_d = lax.dot_general(qd_ref[...], kb.astype(BF16), nt, preferred_element_type=F32)
        s = jnp.concatenate([s_m, s_d], axis=0)
        if bias is not None:
            s = s + bias
        m_old = m_ref[...]
        m_new = jnp.maximum(m_old, jnp.max(s, axis=1, keepdims=True))
        alpha = jnp.exp(m_old - m_new)
        p = jnp.exp(s - m_new)
        l_ref[...] = alpha * l_ref[...] + jnp.sum(p, axis=1, keepdims=True)
        m_ref[...] = m_new
        p_b = p.astype(BF16)
        accm_ref[...] = alpha[:S_ROWS_MLA] * accm_ref[...] + jnp.dot(
            p_b[:S_ROWS_MLA], ckv_b, preferred_element_type=F32)
        accd_ref[...] = alpha[S_ROWS_MLA:] * accd_ref[...] + jnp.dot(
            p_b[S_ROWS_MLA:], vb.astype(BF16), preferred_element_type=F32)

    def pages():
        flat = lambda r: jnp.concatenate([r[:, hh, :] for hh in range(DIFF_HEADS)], axis=1)
        ckv = jnp.concatenate([r[...] for r in ckv_pages], axis=0)
        kpe = jnp.concatenate([r[...] for r in kpe_pages], axis=0)
        kb = jnp.concatenate([flat(r) for r in kb_pages], axis=0)
        vb = jnp.concatenate([flat(r) for r in vb_pages], axis=0)
        return ckv, kpe, kb, vb

    @pl.when(j < last)
    def _():
        process(*pages(), None)

    @pl.when(j == last)
    def _():
        process(*pages(), biasl_ref[...])
        pad = lambda r: jnp.concatenate(
            [r[...], jnp.zeros((LANES - r.shape[0], r.shape[1]), F32)], axis=0)
        process(pad(ckvt_ref), pad(kpet_ref), pad(kbt_ref), pad(vbt_ref), biast_ref[...])

        inv_l = 1.0 / l_ref[...]
        o_lat = (accm_ref[...] * inv_l[:S_ROWS_MLA]).astype(BF16)
        full = jnp.dot(o_lat, wuv_ref[...], preferred_element_type=F32)
        row_h = lax.broadcasted_iota(jnp.int32, full.shape, 0) % MLA_HEADS
        col_h = lax.broadcasted_iota(jnp.int32, full.shape, 1) // MLA_V
        oa = jnp.where(row_h == col_h, full, 0.0).reshape(S_ROWS_MLA // 8, 8, full.shape[1]).sum(axis=1)
        od = accd_ref[...] * inv_l[S_ROWS_MLA:]
        row_h = lax.broadcasted_iota(jnp.int32, od.shape, 0) % 8
        col_h = lax.broadcasted_iota(jnp.int32, od.shape, 1) // DIFF_V
        od = jnp.where(row_h == col_h, od, 0.0).reshape(S_ROWS_DIFF // 8, 8, od.shape[1]).sum(axis=1)
        o1, o2 = od, pltpu.roll(od, 4, 0)
        blocks = []
        for pr in range(MLA_HEADS // 2):
            blk = oa[:, pr * HEAD_PAD:(pr + 1) * HEAD_PAD]
            blocks += [blk, blk]
        for hh in range(DIFF_HEADS):
            blocks += [o1[:, hh * HEAD_PAD:(hh + 1) * HEAD_PAD], o2[:, hh * HEAD_PAD:(hh + 1) * HEAD_PAD]]
        o_ref[...] = jnp.concatenate(blocks, axis=1)


def _sample_attention(page_table, c_ckv, c_kpe, c_kb, c_vb, qg, ones_m, qr, qd, w_uk, w_uv,
                      ckv_t, kpe_t, kb_t, vb_t, bias_last, bias_tail):
    nseq, n_pages = page_table.shape
    P = PAGES_PER_STEP
    assert n_pages % P == 0
    page = c_ckv.shape[2]

    def page_spec(arr, p):
        tail = arr.shape[3:]
        nz = len(tail)
        return pl.BlockSpec((None, None, page) + tail,
                            lambda b, j, pt, p=p, nz=nz: (0, pt[b, j * P + p], 0) + (0,) * nz)

    seq_spec = lambda arr: pl.BlockSpec((None,) + arr.shape[1:], lambda b, j, pt: (b,) + (0,) * (arr.ndim - 1))
    cst_spec = lambda arr: pl.BlockSpec(arr.shape, lambda b, j, pt: (0,) * arr.ndim)
    in_specs = []
    for arr in (c_ckv, c_kpe, c_kb, c_vb):
        in_specs += [page_spec(arr, p) for p in range(P)]
    in_specs += [seq_spec(qg), cst_spec(ones_m), seq_spec(qr), seq_spec(qd), cst_spec(w_uk), cst_spec(w_uv),
                 seq_spec(ckv_t), seq_spec(kpe_t), seq_spec(kb_t), seq_spec(vb_t),
                 cst_spec(bias_last), cst_spec(bias_tail)]
    width = N_UNITS * HEAD_PAD
    grid_spec = pltpu.PrefetchScalarGridSpec(
        num_scalar_prefetch=1, grid=(nseq, n_pages // P), in_specs=in_specs,
        out_specs=pl.BlockSpec((None, 8, width), lambda b, j, pt: (b, 0, 0)),
        scratch_shapes=[pltpu.VMEM((S_ROWS, 1), F32), pltpu.VMEM((S_ROWS, 1), F32),
                        pltpu.VMEM((S_ROWS_MLA, MLA_KV_RANK), F32),
                        pltpu.VMEM((S_ROWS_DIFF, DIFF_HEADS * DIFF_V), F32)])
    args = [c_ckv] * P + [c_kpe] * P + [c_kb] * P + [c_vb] * P
    return pl.pallas_call(
        _sample_attn_kernel, grid_spec=grid_spec,
        out_shape=jax.ShapeDtypeStruct((nseq, 8, width), F32),
        compiler_params=_cparams(("parallel", "arbitrary")), name="sample_attn",
    )(page_table, *args, qg, ones_m, qr, qd, w_uk, w_uv, ckv_t, kpe_t, kb_t, vb_t, bias_last, bias_tail)


def _merge_kernel(x_ref, o_ref, gate_ref, lam_ref, gsub_ref, woa_ref, wob_ref, wo_ref, y_ref):
    o = o_ref[...]
    tm = o.shape[0]
    blk = lambda u: o[:, u * HEAD_PAD:(u + 1) * HEAD_PAD]
    lane = lax.broadcasted_iota(jnp.int32, (tm, HEAD_PAD), 1)
    oa = jnp.concatenate([jnp.where(lane < MLA_V, blk(2 * pr), blk(2 * pr + 1))
                          for pr in range(MLA_HEADS // 2)], axis=1)
    lv = lam_ref[...]
    lam = (jnp.exp(jnp.sum(lv[0:1] * lv[1:2], axis=1, keepdims=True))
           - jnp.exp(jnp.sum(lv[2:3] * lv[3:4], axis=1, keepdims=True)) + LAM_INIT)
    obs = []
    for hh in range(DIFF_HEADS):
        d = blk(MLA_HEADS + 2 * hh) - lam * blk(MLA_HEADS + 2 * hh + 1)
        obs.append(_rms(d, DIFF_V) * gsub_ref[...] * (1.0 - LAM_INIT))
    ob = jnp.concatenate(obs, axis=1)
    g = gate_ref[...]
    d_model = x_ref.shape[1]
    m = (g[:, :d_model] * jnp.dot(oa.astype(BF16), woa_ref[...], preferred_element_type=F32)
         + g[:, d_model:] * jnp.dot(ob.astype(BF16), wob_ref[...], preferred_element_type=F32))
    y_ref[...] = x_ref[...] + jnp.dot(m.astype(BF16), wo_ref[...], preferred_element_type=F32)


def _merge(x2, o2, gate2, lamv, g_sub, w_oa, w_ob, w_o):
    rows, d = x2.shape
    assert rows % ROW_TILE == 0
    row = lambda w: pl.BlockSpec((ROW_TILE, w), lambda i: (i, 0))
    consts = [lamv, g_sub, w_oa, w_ob, w_o]
    return pl.pallas_call(
        _merge_kernel, grid=(rows // ROW_TILE,),
        in_specs=[row(d), row(o2.shape[1]), row(gate2.shape[1])] + [_const_spec(c.shape) for c in consts],
        out_specs=row(d), out_shape=jax.ShapeDtypeStruct((rows, d), F32),
        compiler_params=_cparams(("parallel",)), name="merge",
    )(x2, o2, gate2, *consts)


def _ffn_kernel(x_ref, g_ref, wgu_ref, wdown_ref, y_ref):
    x = x_ref[...]
    hn = _rms(x, x.shape[1]) * g_ref[...]
    gu = jnp.dot(hn.astype(BF16), wgu_ref[...], preferred_element_type=F32)
    dff = gu.shape[1] // 2
    g, u = gu[:, :dff], gu[:, dff:]
    act = g * jax.nn.sigmoid(g) * u
    y_ref[...] = x + jnp.dot(act.astype(BF16), wdown_ref[...], preferred_element_type=F32)


def _ffn(x2, g_ffn, w_gu, w_down):
    rows, d = x2.shape
    assert rows % ROW_TILE == 0
    row = pl.BlockSpec((ROW_TILE, d), lambda i: (i, 0))
    consts = [g_ffn, w_gu, w_down]
    return pl.pallas_call(
        _ffn_kernel, grid=(rows // ROW_TILE,),
        in_specs=[row] + [_const_spec(c.shape) for c in consts],
        out_specs=row, out_shape=jax.ShapeDtypeStruct((rows, d), F32),
        compiler_params=_cparams(("parallel",)), name="ffn",
    )(x2, *consts)


def _rope_tables(pos):
    half = MLA_ROPE // 2
    inv = ROPE_THETA ** (-jnp.arange(half, dtype=F32) / half)
    ang = pos.astype(F32)[:, None] * inv[None, :]
    cos, sin = jnp.cos(ang), jnp.sin(ang)
    t = pos.shape[0]
    one = jnp.ones((t, MLA_NOPE), F32)
    zero = lambda w: jnp.zeros((t, w), F32)
    c = jnp.concatenate([one, cos, cos, jnp.ones((t, HEAD_PAD - MLA_NOPE - MLA_ROPE), F32)], axis=1)
    s_up = jnp.concatenate([zero(MLA_NOPE), -sin, zero(half), zero(HEAD_PAD - MLA_NOPE - MLA_ROPE)], axis=1)
    s_dn = jnp.concatenate([zero(MLA_NOPE), zero(half), sin, zero(HEAD_PAD - MLA_NOPE - MLA_ROPE)], axis=1)
    return jnp.concatenate([c, s_up, s_dn], axis=1)


def _t5_bucket_np(n):
    n = np.maximum(n, 0)
    exact = T5_BUCKETS // 2
    nf = np.maximum(n, 1).astype(np.float32)
    large = exact + (np.log(nf / np.float32(exact)) / np.float32(math.log(T5_MAX_DIST / exact))
                     * np.float32(T5_BUCKETS - exact)).astype(np.int32)
    return np.where(n < exact, n, np.minimum(large, T5_BUCKETS - 1)).astype(np.int32)


def _seg_matrix(segments):
    m = np.zeros((MXU_DIM, MXU_DIM), np.float32)
    for start, length in segments:
        m[start:start + length, start:start + length] = 1.0 / length
    return jnp.asarray(m, BF16)


def _prep_weights(g_attn, w_in, g_cq, w_uq, g_qn, g_qr, g_ckv, g_kr, w_ukv, g_kn, g_qb, g_kb):
    d = w_in.shape[0]
    zcol = lambda w: jnp.zeros((d, w), w_in.dtype)
    off_kpe = MLA_Q_RANK + MLA_KV_RANK
    w_in_p = jnp.concatenate([w_in[:, :off_kpe], zcol(MLA_NOPE), w_in[:, off_kpe:off_kpe + MLA_ROPE],
                              zcol(HEAD_PAD - MLA_NOPE - MLA_ROPE), w_in[:, off_kpe + MLA_ROPE:]], axis=1)
    qk = MLA_NOPE + MLA_ROPE
    w_uq_p = jnp.pad(w_uq.reshape(MLA_Q_RANK, MLA_HEADS, qk), ((0, 0), (0, 0), (0, HEAD_PAD - qk)))
    w_uq_p = w_uq_p.reshape(MLA_Q_RANK, MLA_HEADS * HEAD_PAD)
    w3 = w_ukv.reshape(MLA_KV_RANK, MLA_HEADS, MLA_NOPE + MLA_V)
    w_uk = w3[:, :, :MLA_NOPE]
    w_uv = w3[:, :, MLA_NOPE:].reshape(MLA_KV_RANK, MLA_HEADS * MLA_V)
    w_uk_p = jnp.pad(w_uk, ((0, 0), (0, 0), (0, HEAD_PAD - MLA_NOPE))).reshape(MLA_KV_RANK, MLA_HEADS * HEAD_PAD)
    z = lambda w: jnp.zeros((w,), F32)
    head_q = jnp.concatenate([g_qn, g_qr, z(HEAD_PAD - qk)]) * MLA_SCALE
    head_kn = jnp.concatenate([g_kn, z(HEAD_PAD - MLA_NOPE)])
    row = lambda v: v.reshape(1, -1).astype(F32)
    fw = dict(
        g_attn=row(g_attn), w_in=w_in_p.astype(BF16), g_cq=row(g_cq), w_uq=w_uq_p.astype(BF16),
        g_q=row(jnp.tile(head_q, MLA_HEADS)), g_ckv=row(g_ckv),
        w_kv=jnp.concatenate([w_uk_p, w_uv], axis=1).astype(BF16),
        g_kn=row(jnp.tile(head_kn, MLA_HEADS)),
        g_kpe=row(jnp.concatenate([z(MLA_NOPE), g_kr, z(HEAD_PAD - qk)])),
        g_qb=row(jnp.tile(g_qb.reshape(-1), DIFF_HEADS) * DIFF_SCALE),
        g_kb=row(jnp.tile(g_kb.reshape(-1), DIFF_HEADS)),
        seg_q=_seg_matrix([(0, MLA_NOPE), (MLA_NOPE, MLA_ROPE),
                           (HEAD_PAD, MLA_NOPE), (HEAD_PAD + MLA_NOPE, MLA_ROPE)]),
        seg_b=_seg_matrix([(s, DIFF_QK) for s in range(0, MXU_DIM, DIFF_QK)]),
    )
    return fw, w_uk.reshape(MLA_KV_RANK, MLA_HEADS * MLA_NOPE), w_uv


def _prompt_bias(rel_bias):
    blk = ATT_BLOCK
    r = np.arange(blk)[:, None]
    c = np.arange(blk)[None, :]
    far = rel_bias[T5_BUCKETS - 1]
    diag_b = jnp.transpose(rel_bias[_t5_bucket_np(r - c)] - far, (2, 0, 1))
    prev_b = jnp.transpose(rel_bias[_t5_bucket_np(r - c + blk)] - far, (2, 0, 1))
    causal = jnp.asarray(np.where(c <= r, 0.0, NEG), F32)
    diff = jnp.stack([diag_b + causal, prev_b], axis=1)
    mla = jnp.stack([causal, jnp.zeros((blk, blk), F32)])[None]
    return jnp.concatenate([mla, diff], axis=0).astype(F32)


def _sample_bias(rel_bias, past_len, n_new, page):
    far = rel_bias[T5_BUCKETS - 1]
    rel = jnp.concatenate([rel_bias - far[None, :], jnp.zeros((T5_BUCKETS, 8 - DIFF_HEADS), F32)], axis=1)
    q_d = (np.arange(S_ROWS_DIFF) // 8) % n_new
    h_d = np.arange(S_ROWS_DIFF) % 8
    q_m = np.arange(S_ROWS_MLA) // MLA_HEADS
    kpos = past_len - page + np.arange(page)
    dist = past_len + q_d[:, None] - kpos[None, :]
    last_d = rel[_t5_bucket_np(dist), h_d[:, None]]
    last = jnp.concatenate([jnp.zeros((S_ROWS_MLA, page), F32), last_d], axis=0)
    last = jnp.concatenate([jnp.zeros((S_ROWS, (PAGES_PER_STEP - 1) * page), F32), last], axis=1)
    t = np.arange(LANES)
    tail_d = rel[_t5_bucket_np(q_d[:, None] - t[None, :]), h_d[:, None]]
    vis_d = (t[None, :] < n_new) & (t[None, :] <= q_d[:, None])
    vis_m = (t[None, :] < n_new) & (t[None, :] <= q_m[:, None])
    tail = jnp.concatenate([jnp.where(vis_m, 0.0, NEG).astype(F32),
                            jnp.where(vis_d, tail_d, NEG).astype(F32)], axis=0)
    return last.astype(F32), tail


def _sample_queries(q_s, g_kn, nseq, n_new):
    q3 = q_s.reshape(nseq, n_new, N_UNITS, HEAD_PAD)
    qm = q3[:, :, :MLA_HEADS]
    eye = jnp.eye(MLA_HEADS, dtype=F32)
    qn = qm[..., :MLA_NOPE] * g_kn
    qg = jnp.einsum("bqhd,hg->bqhgd", qn, eye).reshape(nseq, n_new * MLA_HEADS, MLA_HEADS * MLA_NOPE)
    qr = qm[..., MLA_NOPE:MLA_NOPE + MLA_ROPE].reshape(nseq, n_new * MLA_HEADS, MLA_ROPE)
    qd = q3[:, :, MLA_HEADS:].reshape(nseq, n_new, DIFF_HEADS, 2, HEAD_PAD)
    qd = jnp.transpose(qd, (0, 3, 1, 2, 4))
    eye_d = jnp.eye(8, DIFF_HEADS, dtype=F32)
    qd = jnp.einsum("bcqhe,gh->bcqghe", qd, eye_d).reshape(nseq, S_ROWS_DIFF, DIFF_HEADS * HEAD_PAD)
    ones_m = jnp.einsum("qh,hg,d->qhgd", jnp.ones((n_new, MLA_HEADS), F32), eye,
                        jnp.full((MLA_NOPE,), 1.0 / MLA_NOPE, F32)).reshape(n_new * MLA_HEADS, MLA_HEADS * MLA_NOPE)
    n_mla = n_new * MLA_HEADS
    pad_rows = lambda a: jnp.pad(a, [(0, 0)] * (a.ndim - 2) + [(0, S_ROWS_MLA - n_mla), (0, 0)])
    return pad_rows(qg).astype(BF16), pad_rows(ones_m).astype(BF16), pad_rows(qr).astype(BF16), qd.astype(BF16)


def kernel(x_prompt, x_sample, cache_ckv, cache_kpe, cache_kb, cache_vb, page_table, meta_tokens, rel_bias,
           g_attn, w_in, g_cq, w_uq, g_qn, g_qr, g_ckv, g_kr, w_ukv, g_kn, g_qb, g_kb,
           lam_q1, lam_k1, lam_q2, lam_k2, g_sub, w_oa, w_ob, w_o, g_ffn, w_gu, w_down):
    assert w_in.shape[0] == 1, "single-layer trunk"
    nb, seq, d = x_prompt.shape
    nseq, n_new, _ = x_sample.shape
    page = cache_ckv.shape[2]
    past_len = page_table.shape[1] * page
    t_real = seq + N_META
    t_pad = -(-t_real // ATT_BLOCK) * ATT_BLOCK

    fw, w_uk, w_uv = _prep_weights(g_attn[0], w_in[0], g_cq[0], w_uq[0], g_qn[0], g_qr[0], g_ckv[0],
                                   g_kr[0], w_ukv[0], g_kn[0], g_qb[0], g_kb[0])
    lamv = jnp.stack([lam_q1[0], lam_k1[0], lam_q2[0], lam_k2[0]]).astype(F32)
    g_sub_r = g_sub[0].reshape(1, -1).astype(F32)
    w_oa_b, w_ob_b, w_o_b = w_oa[0].astype(BF16), w_ob[0].astype(BF16), w_o[0].astype(BF16)
    g_ffn_r = g_ffn[0].reshape(1, -1).astype(F32)
    w_gu_b, w_down_b = w_gu[0].astype(BF16), w_down[0].astype(BF16)

    def trunk_tail(x2, o2, gate2):
        x1 = _merge(x2, o2, gate2, lamv, g_sub_r, w_oa_b, w_ob_b, w_o_b)
        return _ffn(x1, g_ffn_r, w_gu_b, w_down_b)

    meta = jnp.broadcast_to(meta_tokens[None].astype(x_prompt.dtype), (nb, N_META, d))
    xp = jnp.concatenate([meta, x_prompt, jnp.zeros((nb, t_pad - t_real, d), x_prompt.dtype)], axis=1)
    tab_p = _rope_tables(jnp.arange(t_pad))
    ckv_p, kpe_p, kb_p, vb_p, gate_p, q_p, k_p, v_p = _features(xp, tab_p, fw, BF16)
    o_p = _prompt_attention(q_p, k_p, v_p, _prompt_bias(rel_bias.astype(F32)))
    rows_p = nb * t_pad
    y_p = trunk_tail(xp.reshape(rows_p, d), o_p.reshape(rows_p, -1), gate_p.reshape(rows_p, -1))
    y_prompt = y_p.reshape(nb, t_pad, d)[:, N_META:t_real]

    rows_s = nseq * n_new
    pos_s = past_len + jnp.tile(jnp.arange(n_new), nseq)
    xs = x_sample.reshape(1, rows_s, d)
    ckv_s, kpe_s, kb_s, vb_s, gate_s, q_s, _, _ = _features(xs, _rope_tables(pos_s), fw, F32)
    qg, ones_m, qr, qd = _sample_queries(q_s[0], g_kn[0].astype(F32), nseq, n_new)
    tail = lambda a: jnp.pad(a.reshape(nseq, n_new, -1), ((0, 0), (0, 8 - n_new), (0, 0)))
    kpe_s32 = kpe_s[0][:, MLA_NOPE:MLA_NOPE + MLA_ROPE]
    bias_last, bias_tail = _sample_bias(rel_bias.astype(F32), past_len, n_new, page)
    o_s = _sample_attention(
        page_table, cache_ckv, cache_kpe, cache_kb, cache_vb, qg, ones_m, qr, qd,
        w_uk.astype(BF16), w_uv.astype(BF16),
        tail(ckv_s[0]), tail(kpe_s32), tail(kb_s[0]), tail(vb_s[0]), bias_last, bias_tail)
    o_s2 = o_s[:, :n_new].reshape(rows_s, -1)
    y_s = trunk_tail(x_sample.reshape(rows_s, d), o_s2, gate_s[0])
    y_sample = y_s.reshape(nseq, n_new, d)

    cut = lambda a: a[:, :t_real]
    kpe_p32 = kpe_p[:, :t_real, MLA_NOPE:MLA_NOPE + MLA_ROPE]
    outs_p = (cut(ckv_p)[None], kpe_p32[None],
              cut(kb_p).reshape(nb, t_real, DIFF_HEADS, 2 * DIFF_QK)[None],
              cut(vb_p).reshape(nb, t_real, DIFF_HEADS, DIFF_V)[None])
    outs_s = (ckv_s[0].reshape(nseq, n_new, -1)[None], kpe_s32.reshape(nseq, n_new, -1)[None],
              kb_s[0].reshape(nseq, n_new, DIFF_HEADS, 2 * DIFF_QK)[None],
              vb_s[0].reshape(nseq, n_new, DIFF_HEADS, DIFF_V)[None])
    return (y_prompt, y_sample) + outs_p + outs_s
```

```python
import math

import numpy as np
import jax
import jax.numpy as jnp
from jax import lax
from jax.experimental import pallas as pl
from jax.experimental.pallas import tpu as pltpu

F32 = jnp.float32
BF16 = jnp.bfloat16

N_META = 16
MLA_HEADS = 8
MLA_Q_RANK = 384
MLA_KV_RANK = 256
MLA_NOPE = 64
MLA_ROPE = 32
MLA_V = 64
ROPE_THETA = 10000.0
MLA_SCALE = 1.0 / math.sqrt(MLA_NOPE + MLA_ROPE)
DIFF_HEADS = 4
DIFF_QK = 64
DIFF_V = 2 * DIFF_QK
DIFF_SCALE = 1.0 / math.sqrt(DIFF_QK)
T5_BUCKETS = 32
T5_MAX_DIST = 128
EPS = 1e-6
LAM_INIT = 0.8 - 0.6 * math.exp(-0.3 * 0)
LOG2E = math.log2(math.e)

LANES = 128
SUBLANES = 8
MXU_DIM = 256
VMEM_LIMIT_BYTES = 56 * 1024 * 1024

ROW_TILE = 256
ATT_BLOCK = 512
MLA_GROUP = 4
DIFF_GROUP = 2
PAGES_PER_STEP = 8
NEG = -1e30

HEAD_PAD = 128
Q_WIDTH = (MLA_HEADS + 2 * DIFF_HEADS) * HEAD_PAD
K_WIDTH = (MLA_HEADS + DIFF_HEADS) * HEAD_PAD
V_WIDTH = MLA_HEADS * MLA_V + DIFF_HEADS * DIFF_V

ZC_CQ = 0
ZC_CKV = ZC_CQ + MLA_Q_RANK
ZC_KPE = ZC_CKV + MLA_KV_RANK
ZC_QB = ZC_KPE + HEAD_PAD
ZC_KB = ZC_QB + DIFF_HEADS * 2 * DIFF_QK
ZC_VB = ZC_KB + DIFF_HEADS * 2 * DIFF_QK
ZC_GATE = ZC_VB + DIFF_HEADS * DIFF_V

NT = (((1,), (1,)), ((), ()))


def _cparams(sem):
    return pltpu.CompilerParams(dimension_semantics=sem, vmem_limit_bytes=VMEM_LIMIT_BYTES)


def _const_spec(shape):
    nd = len(shape)
    return pl.BlockSpec(shape, lambda *_: (0,) * nd)


def _rms(x, n):
    ss = jnp.sum(x * x, axis=-1, keepdims=True)
    return x * lax.rsqrt(ss * (1.0 / n) + EPS)


def _seg_rms(x, seg_mat):
    outs = []
    for c in range(x.shape[1] // MXU_DIM):
        xc = x[:, c * MXU_DIM:(c + 1) * MXU_DIM]
        ms = jnp.dot((xc * xc).astype(BF16), seg_mat, preferred_element_type=F32)
        outs.append(xc * lax.rsqrt(ms + EPS))
    return jnp.concatenate(outs, axis=1)


def _rope(x, c, s_up, s_dn):
    w = x.shape[1]
    return x * c + pltpu.roll(x, w - MLA_ROPE // 2, 1) * s_up + pltpu.roll(x, MLA_ROPE // 2, 1) * s_dn


def _lam(lam_ref):
    lv = lam_ref[...]
    return (jnp.exp(jnp.sum(lv[0:1] * lv[1:2], axis=1, keepdims=True))
            - jnp.exp(jnp.sum(lv[2:3] * lv[3:4], axis=1, keepdims=True)) + LAM_INIT)


def _feature_kernel(x_ref, tab_ref, gattn_ref, win_ref, gcq_ref, wuq_ref, gq_ref, gckv_ref, wkv_ref,
                    gkn_ref, gkpe_ref, gqb_ref, gkb_ref, sq_ref, sb_ref,
                    ckv_ref, kpe_ref, kb_ref, vb_ref, gate_ref, q_ref, k_ref, vt_ref):
    x = x_ref[...]
    h = _rms(x, x.shape[1]) * gattn_ref[...]
    z = jnp.dot(h.astype(BF16), win_ref[...], preferred_element_type=F32)

    tab = tab_ref[...]
    rc, rs_up, rs_dn = tab[:, 0:LANES], tab[:, LANES:2 * LANES], tab[:, 2 * LANES:3 * LANES]
    tile8 = lambda a: jnp.concatenate([a] * MLA_HEADS, axis=1)

    gate_ref[...] = jax.nn.sigmoid(z[:, ZC_GATE:])

    ckv = _rms(z[:, ZC_CKV:ZC_KPE], MLA_KV_RANK) * gckv_ref[...]
    ckv_ref[...] = ckv
    kv = jnp.dot(ckv.astype(BF16), wkv_ref[...], preferred_element_type=F32)
    kn = _seg_rms(kv[:, :MLA_HEADS * HEAD_PAD], sq_ref[...]) * gkn_ref[...]

    kp = z[:, ZC_KPE:ZC_QB]
    kp = _rms(kp, MLA_ROPE) * gkpe_ref[...]
    kp = _rope(kp, rc, rs_up, rs_dn)
    kpe_ref[...] = kp
    k_ref[:, :MLA_HEADS * HEAD_PAD] = (kn + tile8(kp)).astype(k_ref.dtype)

    cq = _rms(z[:, ZC_CQ:ZC_CKV], MLA_Q_RANK) * gcq_ref[...]
    q = jnp.dot(cq.astype(BF16), wuq_ref[...], preferred_element_type=F32)
    q = _seg_rms(q, sq_ref[...]) * gq_ref[...]
    q = _rope(q, tile8(rc), tile8(rs_up), tile8(rs_dn))
    q_ref[:, :MLA_HEADS * HEAD_PAD] = q.astype(q_ref.dtype)

    qb = _seg_rms(z[:, ZC_QB:ZC_KB], sb_ref[...]) * gqb_ref[...]
    lane = lax.broadcasted_iota(jnp.int32, (x.shape[0], HEAD_PAD), 1)
    first = lane < DIFF_QK
    parts = []
    for hh in range(DIFF_HEADS):
        qh = qb[:, hh * HEAD_PAD:(hh + 1) * HEAD_PAD]
        parts.append(jnp.where(first, qh, 0.0))
        parts.append(jnp.where(first, 0.0, qh))
    q_ref[:, MLA_HEADS * HEAD_PAD:] = jnp.concatenate(parts, axis=1).astype(q_ref.dtype)

    kb = _seg_rms(z[:, ZC_KB:ZC_VB], sb_ref[...]) * gkb_ref[...]
    kb_ref[...] = kb
    k_ref[:, MLA_HEADS * HEAD_PAD:] = kb.astype(k_ref.dtype)
    vb = z[:, ZC_VB:ZC_GATE]
    vb_ref[...] = vb
    v_all = jnp.concatenate([kv[:, MLA_HEADS * HEAD_PAD:], vb], axis=1)
    vt_ref[...] = jnp.transpose(v_all).astype(vt_ref.dtype)


def _features(x3, tab, fw, q_dtype):
    nb, t, d = x3.shape
    assert t % ROW_TILE == 0
    grid = (nb, t // ROW_TILE)
    row = lambda w: pl.BlockSpec((None, ROW_TILE, w), lambda b, i: (b, i, 0))
    consts = [fw["g_attn"], fw["w_in"], fw["g_cq"], fw["w_uq"], fw["g_q"], fw["g_ckv"], fw["w_kv"],
              fw["g_kn"], fw["g_kpe"], fw["g_qb"], fw["g_kb"], fw["seg_q"], fw["seg_b"]]
    in_specs = [row(d), pl.BlockSpec((ROW_TILE, 3 * LANES), lambda b, i: (i, 0))]
    in_specs += [_const_spec(c.shape) for c in consts]
    widths = [(MLA_KV_RANK, F32), (HEAD_PAD, F32), (DIFF_HEADS * 2 * DIFF_QK, F32),
              (DIFF_HEADS * DIFF_V, F32), (2 * d, F32), (Q_WIDTH, q_dtype), (K_WIDTH, BF16)]
    out_shape = [jax.ShapeDtypeStruct((nb, t, w), dt) for w, dt in widths]
    out_specs = [row(w) for w, _ in widths]
    out_shape.append(jax.ShapeDtypeStruct((nb, V_WIDTH, t), BF16))
    out_specs.append(pl.BlockSpec((None, V_WIDTH, ROW_TILE), lambda b, i: (b, 0, i)))
    return pl.pallas_call(
        _feature_kernel, grid=grid, in_specs=in_specs, out_specs=out_specs, out_shape=out_shape,
        compiler_params=_cparams(("parallel", "parallel")), name="features",
    )(x3, tab, *consts)


def _softmax_block(s, m_old, l_old, acc_ref, u, vt):
    m_new = jnp.maximum(m_old, jnp.max(s, axis=0, keepdims=True))
    alpha = jnp.exp2(m_old - m_new)
    p = jnp.exp2(s - m_new)
    l_new = alpha * l_old + jnp.sum(p, axis=0, keepdims=True)
    acc_ref[u] = alpha * acc_ref[u] + jnp.dot(vt, p.astype(BF16), preferred_element_type=F32)
    return m_new, l_new


def _meta_init(s, acc_ref, u, vtm):
    m = jnp.max(s, axis=0, keepdims=True)
    p = jnp.exp2(s - m)
    acc_ref[u] = jnp.dot(vtm, p.astype(BF16), preferred_element_type=F32)
    return m, jnp.sum(p, axis=0, keepdims=True)


def _mla_attn_kernel(q_ref, k_ref, vt_ref, km_ref, vtm_ref, mask_ref, o_ref, acc_ref):
    i = pl.program_id(2)
    blk = ATT_BLOCK
    units = range(MLA_GROUP)
    qs = [q_ref[:, u * HEAD_PAD:(u + 1) * HEAD_PAD] for u in units]
    dv = MLA_V

    ms, ls = [], []
    for u in units:
        s = lax.dot_general(km_ref[:, u * HEAD_PAD:(u + 1) * HEAD_PAD], qs[u], NT, preferred_element_type=F32)
        m, l = _meta_init(s, acc_ref, u, vtm_ref[u * dv:(u + 1) * dv, :])
        ms.append(m)
        ls.append(l)

    def block(j, carry, mask):
        ms, ls = carry
        start = pl.multiple_of(j * blk, blk)
        out_m, out_l = [], []
        for u in units:
            k = k_ref[pl.ds(start, blk), u * HEAD_PAD:(u + 1) * HEAD_PAD]
            s = lax.dot_general(k, qs[u], NT, preferred_element_type=F32)
            if mask is not None:
                s = s + mask
            m, l = _softmax_block(s, ms[u], ls[u], acc_ref, u, vt_ref[u * dv:(u + 1) * dv, pl.ds(start, blk)])
            out_m.append(m)
            out_l.append(l)
        return tuple(out_m), tuple(out_l)

    carry = lax.fori_loop(0, i, lambda j, c: block(j, c, None), (tuple(ms), tuple(ls)))
    ms, ls = block(i, carry, mask_ref[...])
    o_t = jnp.concatenate([acc_ref[u] * (1.0 / ls[u]) for u in units], axis=0)
    o_ref[...] = jnp.transpose(o_t)


def _diff_attn_kernel(q_ref, k_ref, vt_ref, km_ref, vtm_ref, bdiag_ref, bprev_ref, bmeta_ref,
                      lam_ref, gsub_ref, o_ref, acc_ref):
    i = pl.program_id(2)
    blk = ATT_BLOCK
    dv = DIFF_V
    units = [(hh, c) for hh in range(DIFF_GROUP) for c in range(2)]
    qs = [q_ref[:, u * HEAD_PAD:(u + 1) * HEAD_PAD] for u in range(len(units))]

    first = i == 0
    ms, ls = [], []
    for u, (hh, c) in enumerate(units):
        s = lax.dot_general(km_ref[:, hh * HEAD_PAD:(hh + 1) * HEAD_PAD], qs[u], NT, preferred_element_type=F32)
        s = s + jnp.where(first, bmeta_ref[hh], 0.0)
        m, l = _meta_init(s, acc_ref, u, vtm_ref[hh * dv:(hh + 1) * dv, :])
        ms.append(m)
        ls.append(l)

    def block(j, carry, bias_ref):
        ms, ls = carry
        start = pl.multiple_of(j * blk, blk)
        out_m, out_l = [], []
        for u, (hh, c) in enumerate(units):
            k = k_ref[pl.ds(start, blk), hh * HEAD_PAD:(hh + 1) * HEAD_PAD]
            s = lax.dot_general(k, qs[u], NT, preferred_element_type=F32)
            if bias_ref is not None:
                s = s + bias_ref[hh]
            m, l = _softmax_block(s, ms[u], ls[u], acc_ref, u, vt_ref[hh * dv:(hh + 1) * dv, pl.ds(start, blk)])
            out_m.append(m)
            out_l.append(l)
        return tuple(out_m), tuple(out_l)

    carry = lax.fori_loop(0, jnp.maximum(i - 1, 0), lambda j, c: block(j, c, None), (tuple(ms), tuple(ls)))
    carry = lax.cond(i >= 1, lambda c: block(i - 1, c, bprev_ref), lambda c: c, carry)
    ms, ls = block(i, carry, bdiag_ref)

    lam = _lam(lam_ref)
    outs = []
    for hh in range(DIFF_GROUP):
        o1 = acc_ref[2 * hh] * (1.0 / ls[2 * hh])
        o2 = acc_ref[2 * hh + 1] * (1.0 / ls[2 * hh + 1])
        d = o1 - lam * o2
        ms2 = jnp.sum(d * d, axis=0, keepdims=True) * (1.0 / dv)
        outs.append(d * lax.rsqrt(ms2 + EPS))
    o = jnp.transpose(jnp.concatenate(outs, axis=0))
    g = gsub_ref[...] * (1.0 - LAM_INIT)
    o_ref[...] = o * jnp.concatenate([g] * DIFF_GROUP, axis=1)


def _prompt_attention(q_all, k_all, vt_all, k_meta, vt_meta, masks, lamv, g_sub):
    nb, t, _ = q_all.shape
    assert t % ATT_BLOCK == 0
    nq = t // ATT_BLOCK
    blk = ATT_BLOCK
    n_meta = k_meta.shape[0]
    sem = _cparams(("parallel", "arbitrary", "arbitrary"))

    gw = MLA_GROUP * HEAD_PAD
    gv = MLA_GROUP * MLA_V
    oa = pl.pallas_call(
        _mla_attn_kernel, grid=(nb, MLA_HEADS // MLA_GROUP, nq),
        in_specs=[pl.BlockSpec((None, blk, gw), lambda b, g, i: (b, i, g)),
                  pl.BlockSpec((None, t, gw), lambda b, g, i: (b, 0, g)),
                  pl.BlockSpec((None, gv, t), lambda b, g, i: (b, g, 0)),
                  pl.BlockSpec((n_meta, gw), lambda b, g, i: (0, g)),
                  pl.BlockSpec((gv, n_meta), lambda b, g, i: (g, 0)),
                  _const_spec(masks["causal"].shape)],
        out_specs=pl.BlockSpec((None, blk, gv), lambda b, g, i: (b, i, g)),
        out_shape=jax.ShapeDtypeStruct((nb, t, MLA_HEADS * MLA_V), F32),
        scratch_shapes=[pltpu.VMEM((MLA_GROUP, MLA_V, blk), F32)],
        compiler_params=sem, name="mla_attn",
    )(q_all, k_all, vt_all, k_meta, vt_meta, masks["causal"])

    dq = 2 * DIFF_GROUP * HEAD_PAD
    dk = DIFF_GROUP * HEAD_PAD
    dvw = DIFF_GROUP * DIFF_V
    q_off = MLA_HEADS * HEAD_PAD // dq
    k_off = MLA_HEADS * HEAD_PAD // dk
    v_off = MLA_HEADS * MLA_V // dvw
    bias_spec = lambda a: pl.BlockSpec((DIFF_GROUP,) + a.shape[1:], lambda b, g, i: (g, 0, 0))
    ob = pl.pallas_call(
        _diff_attn_kernel, grid=(nb, DIFF_HEADS // DIFF_GROUP, nq),
        in_specs=[pl.BlockSpec((None, blk, dq), lambda b, g, i: (b, i, q_off + g)),
                  pl.BlockSpec((None, t, dk), lambda b, g, i: (b, 0, k_off + g)),
                  pl.BlockSpec((None, dvw, t), lambda b, g, i: (b, v_off + g, 0)),
                  pl.BlockSpec((n_meta, dk), lambda b, g, i: (0, k_off + g)),
                  pl.BlockSpec((dvw, n_meta), lambda b, g, i: (v_off + g, 0)),
                  bias_spec(masks["diag"]), bias_spec(masks["prev"]), bias_spec(masks["meta"]),
                  _const_spec(lamv.shape), _const_spec(g_sub.shape)],
        out_specs=pl.BlockSpec((None, blk, dvw), lambda b, g, i: (b, i, g)),
        out_shape=jax.ShapeDtypeStruct((nb, t, DIFF_HEADS * DIFF_V), F32),
        scratch_shapes=[pltpu.VMEM((2 * DIFF_GROUP, DIFF_V, blk), F32)],
        compiler_params=sem, name="diff_attn",
    )(q_all, k_all, vt_all, k_meta, vt_meta, masks["diag"], masks["prev"], masks["meta"], lamv, g_sub)
    return oa, ob


S_ROWS = 64
TAIL_KEYS_MLA = 128
TAIL_KEYS_DIFF = 32


def _sample_attn_kernel(pt_ref, *refs):
    P = PAGES_PER_STEP
    ckv_pages = refs[0:P]
    kpet_pages = refs[P:2 * P]
    kb_pages = refs[2 * P:3 * P]
    vb_pages = refs[3 * P:4 * P]
    (qg_ref, qr_ref, qd_ref, ones_ref, wuk_ref, wuv_ref, hmask_ref, hmaskl_ref,
     ckvt_ref, kpett_ref, kbt_ref, vbt_ref, tmaskm_ref, tmaskd_ref, lam_ref, gsub_ref,
     oa_ref, ob_ref, mm_ref, lm_ref, md_ref, ld_ref, accm_ref, accd_ref) = refs[4 * P:]
    j = pl.program_id(1)
    last = pl.num_programs(1) - 1

    @pl.when(j == 0)
    def _():
        for r in (mm_ref, md_ref):
            r[...] = jnp.full(r.shape, NEG, F32)
        for r in (lm_ref, ld_ref, accm_ref, accd_ref):
            r[...] = jnp.zeros(r.shape, F32)

    def update(s, m_ref, l_ref, acc_ref, v_b):
        m_old = m_ref[...]
        m_new = jnp.maximum(m_old, jnp.max(s, axis=1, keepdims=True))
        alpha = jnp.exp2(m_old - m_new)
        p = jnp.exp2(s - m_new)
        l_ref[...] = alpha * l_ref[...] + jnp.sum(p, axis=1, keepdims=True)
        m_ref[...] = m_new
        acc_ref[...] = alpha * acc_ref[...] + jnp.dot(p.astype(BF16), v_b, preferred_element_type=F32)

    def mla_block(ckv, kpet, mask):
        ckv_b = ckv.astype(BF16)
        kexp = jnp.dot(ckv_b, wuk_ref[...], preferred_element_type=F32)
        s = lax.dot_general(qg_ref[...], kexp.astype(BF16), NT, preferred_element_type=F32)
        ms = lax.dot_general(ones_ref[...], (kexp * kexp).astype(BF16), NT, preferred_element_type=F32)
        s = s * lax.rsqrt(ms + EPS) + jnp.dot(qr_ref[...], kpet.astype(BF16), preferred_element_type=F32)
        if mask is not None:
            s = s + mask
        update(s, mm_ref, lm_ref, accm_ref, ckv_b)

    def diff_block(kb, vb, mask):
        s = lax.dot_general(qd_ref[...], kb.astype(BF16), NT, preferred_element_type=F32) + mask
        update(s, md_ref, ld_ref, accd_ref, vb.astype(BF16))

    def pages(hmask):
        mla_block(jnp.concatenate([r[...] for r in ckv_pages], axis=0),
                  jnp.concatenate([r[...] for r in kpet_pages], axis=1), None)
        diff_block(jnp.concatenate([r[...] for r in kb_pages], axis=0),
                   jnp.concatenate([r[...] for r in vb_pages], axis=0), hmask)

    @pl.when(j < last)
    def _():
        pages(hmask_ref[...])

    @pl.when(j == last)
    def _():
        pages(hmaskl_ref[...])
        pad_rows = lambda r, n: jnp.concatenate([r[...], jnp.zeros((n - r.shape[0], r.shape[1]), F32)], axis=0)
        mla_block(pad_rows(ckvt_ref, TAIL_KEYS_MLA), kpett_ref[...], tmaskm_ref[...])
        rows_d = TAIL_KEYS_DIFF * DIFF_HEADS
        diff_block(pad_rows(kbt_ref, rows_d), pad_rows(vbt_ref, rows_d), tmaskd_ref[...])

        o_lat = (accm_ref[...] * (1.0 / lm_ref[...])).astype(BF16)
        full = jnp.dot(o_lat, wuv_ref[...], preferred_element_type=F32)
        col_h = lax.broadcasted_iota(jnp.int32, (SUBLANES, full.shape[1]), 1) // MLA_V
        oa = jnp.zeros((SUBLANES, full.shape[1]), F32)
        for hh in range(MLA_HEADS):
            oa = oa + jnp.where(col_h == hh, full[hh * SUBLANES:(hh + 1) * SUBLANES], 0.0)
        oa_ref[...] = oa
        od = accd_ref[...] * (1.0 / ld_ref[...])
        lam = _lam(lam_ref)
        g = gsub_ref[...] * (1.0 - LAM_INIT)
        outs = []
        for hh in range(DIFF_HEADS):
            o1 = od[hh * SUBLANES:(hh + 1) * SUBLANES]
            o2 = od[(DIFF_HEADS + hh) * SUBLANES:(DIFF_HEADS + hh + 1) * SUBLANES]
            outs.append(_rms(o1 - lam * o2, DIFF_V) * g)
        ob_ref[...] = jnp.concatenate(outs, axis=1)


def _sample_attention(page_table, c_ckv, c_kpet, c_kb, c_vb, seq_ops, const_ops, tail_ops, tail_masks, lamv, g_sub):
    nseq, n_pages = page_table.shape
    P = PAGES_PER_STEP
    assert n_pages % P == 0

    def page_spec(arr, p):
        return pl.BlockSpec((None, None) + arr.shape[2:], lambda b, j, pt, p=p: (0, pt[b, j * P + p], 0, 0))

    seq_spec = lambda a: pl.BlockSpec((None,) + a.shape[1:], lambda b, j, pt: (b,) + (0,) * (a.ndim - 1))
    cst_spec = lambda a: pl.BlockSpec(a.shape, lambda b, j, pt: (0,) * a.ndim)
    in_specs = []
    for arr in (c_ckv, c_kpet, c_kb, c_vb):
        in_specs += [page_spec(arr, p) for p in range(P)]
    in_specs += [seq_spec(a) for a in seq_ops] + [cst_spec(a) for a in const_ops]
    in_specs += [seq_spec(a) for a in tail_ops] + [cst_spec(a) for a in tail_masks]
    in_specs += [cst_spec(lamv), cst_spec(g_sub)]
    out_w = MLA_HEADS * MLA_V
    out_spec = pl.BlockSpec((None, SUBLANES, out_w), lambda b, j, pt: (b, 0, 0))
    col = lambda w: pltpu.VMEM((S_ROWS, w), F32)
    grid_spec = pltpu.PrefetchScalarGridSpec(
        num_scalar_prefetch=1, grid=(nseq, n_pages // P), in_specs=in_specs,
        out_specs=[out_spec, out_spec],
        scratch_shapes=[col(1), col(1), col(1), col(1), col(MLA_KV_RANK), col(DIFF_V)])
    args = [c_ckv] * P + [c_kpet] * P + [c_kb] * P + [c_vb] * P
    out = jax.ShapeDtypeStruct((nseq, SUBLANES, out_w), F32)
    return pl.pallas_call(
        _sample_attn_kernel, grid_spec=grid_spec, out_shape=[out, out],
        compiler_params=_cparams(("parallel", "arbitrary")), name="sample_attn",
    )(page_table, *args, *seq_ops, *const_ops, *tail_ops, *tail_masks, lamv, g_sub)


def _merge_kernel(x_ref, oa_ref, ob_ref, gate_ref, woa_ref, wob_ref, wo_ref, y_ref):
    g = gate_ref[...]
    d_model = x_ref.shape[1]
    m = (g[:, :d_model] * jnp.dot(oa_ref[...].astype(BF16), woa_ref[...], preferred_element_type=F32)
         + g[:, d_model:] * jnp.dot(ob_ref[...].astype(BF16), wob_ref[...], preferred_element_type=F32))
    y_ref[...] = x_ref[...] + jnp.dot(m.astype(BF16), wo_ref[...], preferred_element_type=F32)


def _merge(x2, oa2, ob2, gate2, w_oa, w_ob, w_o):
    rows, d = x2.shape
    assert rows % ROW_TILE == 0
    row = lambda w: pl.BlockSpec((ROW_TILE, w), lambda i: (i, 0))
    consts = [w_oa, w_ob, w_o]
    return pl.pallas_call(
        _merge_kernel, grid=(rows // ROW_TILE,),
        in_specs=[row(d), row(oa2.shape[1]), row(ob2.shape[1]), row(gate2.shape[1])]
        + [_const_spec(c.shape) for c in consts],
        out_specs=row(d), out_shape=jax.ShapeDtypeStruct((rows, d), F32),
        compiler_params=_cparams(("parallel",)), name="merge",
    )(x2, oa2, ob2, gate2, *consts)


def _ffn_kernel(x_ref, g_ref, wgu_ref, wdown_ref, y_ref):
    x = x_ref[...]
    hn = _rms(x, x.shape[1]) * g_ref[...]
    gu = jnp.dot(hn.astype(BF16), wgu_ref[...], preferred_element_type=F32)
    dff = gu.shape[1] // 2
    g, u = gu[:, :dff], gu[:, dff:]
    act = g * jax.nn.sigmoid(g) * u
    y_ref[...] = x + jnp.dot(act.astype(BF16), wdown_ref[...], preferred_element_type=F32)


def _ffn(x2, g_ffn, w_gu, w_down):
    rows, d = x2.shape
    assert rows % ROW_TILE == 0
    row = pl.BlockSpec((ROW_TILE, d), lambda i: (i, 0))
    consts = [g_ffn, w_gu, w_down]
    return pl.pallas_call(
        _ffn_kernel, grid=(rows // ROW_TILE,),
        in_specs=[row] + [_const_spec(c.shape) for c in consts],
        out_specs=row, out_shape=jax.ShapeDtypeStruct((rows, d), F32),
        compiler_params=_cparams(("parallel",)), name="ffn",
    )(x2, *consts)


def _rope_tables(pos):
    half = MLA_ROPE // 2
    inv = ROPE_THETA ** (-jnp.arange(half, dtype=F32) / half)
    ang = pos.astype(F32)[:, None] * inv[None, :]
    cos, sin = jnp.cos(ang), jnp.sin(ang)
    t = pos.shape[0]
    rest = HEAD_PAD - MLA_NOPE - MLA_ROPE
    one = lambda w: jnp.ones((t, w), F32)
    zero = lambda w: jnp.zeros((t, w), F32)
    c = jnp.concatenate([one(MLA_NOPE), cos, cos, one(rest)], axis=1)
    s_up = jnp.concatenate([zero(MLA_NOPE), -sin, zero(half), zero(rest)], axis=1)
    s_dn = jnp.concatenate([zero(MLA_NOPE), zero(half), sin, zero(rest)], axis=1)
    return jnp.concatenate([c, s_up, s_dn], axis=1)


def _t5_bucket_np(n):
    n = np.maximum(n, 0)
    exact = T5_BUCKETS // 2
    nf = np.maximum(n, 1).astype(np.float32)
    large = exact + (np.log(nf / np.float32(exact)) / np.float32(math.log(T5_MAX_DIST / exact))
                     * np.float32(T5_BUCKETS - exact)).astype(np.int32)
    return np.where(n < exact, n, np.minimum(large, T5_BUCKETS - 1)).astype(np.int32)


def _rel_by_distance(rel_bias, n):
    far = rel_bias[T5_BUCKETS - 1]
    onehot = jnp.asarray(np.eye(T5_BUCKETS, dtype=np.float32)[_t5_bucket_np(np.arange(n))])
    by_dist = jnp.dot(onehot, rel_bias, precision=lax.Precision.HIGHEST)
    return jnp.transpose(by_dist - far[None, :]) * LOG2E


def _toeplitz_t(u, n):
    lead = u.shape[:-1]
    tiled = jnp.tile(u, (1,) * len(lead) + (n,))[..., :n * (2 * n - 1)]
    return tiled.reshape(lead + (n, 2 * n - 1))[..., :n]


def _prompt_masks(rel_bias, n_meta):
    n = ATT_BLOCK
    val = _rel_by_distance(rel_bias, 2 * n)
    h = val.shape[0]
    neg = jnp.full((h, n), NEG, F32)
    diag = _toeplitz_t(jnp.concatenate([val[:, :n], neg], axis=1), n)
    prev = _toeplitz_t(jnp.concatenate([val[:, n:], val[:, :n]], axis=1), n)
    causal = _toeplitz_t(jnp.concatenate([jnp.zeros((n,), F32), jnp.full((n,), NEG, F32)]), n)
    val_m = _rel_by_distance(rel_bias, n_meta + n)
    meta = jnp.stack([val_m[:, n_meta - m:n_meta - m + n] for m in range(n_meta)], axis=1)
    return dict(causal=causal, diag=diag, prev=prev, meta=meta)


def _seg_matrix(segments):
    m = np.zeros((MXU_DIM, MXU_DIM), np.float32)
    for start, length in segments:
        m[start:start + length, start:start + length] = 1.0 / length
    return jnp.asarray(m, BF16)


def _prep_weights(g_attn, w_in, g_cq, w_uq, g_qn, g_qr, g_ckv, g_kr, w_ukv, g_kn, g_qb, g_kb):
    d = w_in.shape[0]
    zcol = lambda w: jnp.zeros((d, w), w_in.dtype)
    off_kpe = MLA_Q_RANK + MLA_KV_RANK
    w_in_p = jnp.concatenate([w_in[:, :off_kpe], zcol(MLA_NOPE), w_in[:, off_kpe:off_kpe + MLA_ROPE],
                              zcol(HEAD_PAD - MLA_NOPE - MLA_ROPE), w_in[:, off_kpe + MLA_ROPE:]], axis=1)
    qk = MLA_NOPE + MLA_ROPE
    w_uq_p = jnp.pad(w_uq.reshape(MLA_Q_RANK, MLA_HEADS, qk), ((0, 0), (0, 0), (0, HEAD_PAD - qk)))
    w_uq_p = w_uq_p.reshape(MLA_Q_RANK, MLA_HEADS * HEAD_PAD)
    w3 = w_ukv.reshape(MLA_KV_RANK, MLA_HEADS, MLA_NOPE + MLA_V)
    w_uk = w3[:, :, :MLA_NOPE]
    w_uv = w3[:, :, MLA_NOPE:].reshape(MLA_KV_RANK, MLA_HEADS * MLA_V)
    w_uk_p = jnp.pad(w_uk, ((0, 0), (0, 0), (0, HEAD_PAD - MLA_NOPE))).reshape(MLA_KV_RANK, MLA_HEADS * HEAD_PAD)
    z = lambda w: jnp.zeros((w,), F32)
    head_q = jnp.concatenate([g_qn, g_qr, z(HEAD_PAD - qk)]) * (MLA_SCALE * LOG2E)
    head_kn = jnp.concatenate([g_kn, z(HEAD_PAD - MLA_NOPE)])
    row = lambda v: v.reshape(1, -1).astype(F32)
    fw = dict(
        g_attn=row(g_attn), w_in=w_in_p.astype(BF16), g_cq=row(g_cq), w_uq=w_uq_p.astype(BF16),
        g_q=row(jnp.tile(head_q, MLA_HEADS)), g_ckv=row(g_ckv),
        w_kv=jnp.concatenate([w_uk_p, w_uv], axis=1).astype(BF16),
        g_kn=row(jnp.tile(head_kn, MLA_HEADS)),
        g_kpe=row(jnp.concatenate([z(MLA_NOPE), g_kr, z(HEAD_PAD - qk)])),
        g_qb=row(jnp.tile(g_qb.reshape(-1), DIFF_HEADS) * (DIFF_SCALE * LOG2E)),
        g_kb=row(jnp.tile(g_kb.reshape(-1), DIFF_HEADS)),
        seg_q=_seg_matrix([(0, MLA_NOPE), (MLA_NOPE, MLA_ROPE),
                           (HEAD_PAD, MLA_NOPE), (HEAD_PAD + MLA_NOPE, MLA_ROPE)]),
        seg_b=_seg_matrix([(s, DIFF_QK) for s in range(0, MXU_DIM, DIFF_QK)]),
    )
    return fw, w_uk.reshape(MLA_KV_RANK, MLA_HEADS * MLA_NOPE), w_uv


def _sample_operands(q_s, g_kn, nseq, n_new):
    q4 = q_s.reshape(nseq, n_new, Q_WIDTH // HEAD_PAD, HEAD_PAD)
    padq = lambda a, ax: jnp.pad(a, [(0, SUBLANES - n_new) if k == ax else (0, 0) for k in range(a.ndim)])
    qm = jnp.transpose(q4[:, :, :MLA_HEADS], (0, 2, 1, 3))
    eye = jnp.eye(MLA_HEADS, dtype=F32)
    qn = qm[..., :MLA_NOPE] * g_kn
    qg = padq(jnp.einsum("bhqd,hg->bhqgd", qn, eye), 2).reshape(nseq, S_ROWS, MLA_HEADS * MLA_NOPE)
    qr = padq(qm[..., MLA_NOPE:MLA_NOPE + MLA_ROPE], 2).reshape(nseq, S_ROWS, MLA_ROPE)
    qd = q4[:, :, MLA_HEADS:].reshape(nseq, n_new, DIFF_HEADS, 2, HEAD_PAD)
    qd = padq(jnp.transpose(qd, (0, 3, 2, 1, 4)), 3).reshape(nseq, S_ROWS, HEAD_PAD)
    ones_m = jnp.einsum("hq,hg,d->hqgd", jnp.ones((MLA_HEADS, SUBLANES), F32), eye,
                        jnp.full((MLA_NOPE,), 1.0 / MLA_NOPE, F32)).reshape(S_ROWS, MLA_HEADS * MLA_NOPE)
    return qg.astype(BF16), qr.astype(BF16), qd.astype(BF16), ones_m.astype(BF16)


def _sample_masks(rel_bias, past_len, n_new, page):
    assert page >= T5_MAX_DIST, "only the last cached page may hold keys with a non-saturated bias"
    rel = _rel_by_distance(rel_bias, page + SUBLANES)
    nh = DIFF_HEADS
    row = np.arange(S_ROWS)
    row_h = (row // SUBLANES) % nh
    ncol = PAGES_PER_STEP * page * nh
    own = (np.arange(ncol)[None, :] % nh) == row_h[:, None]
    hmask = jnp.asarray(np.where(own, 0.0, NEG), F32)
    eye = np.eye(nh, dtype=bool)

    def expand(a, vis):
        keep = eye[:, None, None, :] & vis[None, :, :, None]
        b = jnp.where(keep, a[..., None], NEG)
        return jnp.tile(b.reshape(1, nh * SUBLANES, -1), (2, 1, 1)).reshape(S_ROWS, -1).astype(F32)

    qs = [min(q, n_new - 1) for q in range(SUBLANES)]
    a_last = jnp.stack([rel[:, q + 1:q + 1 + page][:, ::-1] for q in qs], axis=1)
    last_page = expand(a_last, np.ones((SUBLANES, page), bool))
    hmask_last = jnp.concatenate([hmask[:, :ncol - page * nh], last_page], axis=1)
    q8 = np.arange(SUBLANES)
    t = np.arange(TAIL_KEYS_MLA)
    vis_m = (t[None, :] < n_new) & (t[None, :] <= q8[:, None])
    tmask_m = jnp.asarray(np.tile(np.where(vis_m, 0.0, NEG), (MLA_HEADS, 1)), F32)
    tk = np.arange(TAIL_KEYS_DIFF)
    onehot = (q8[:, None, None] - tk[None, :, None] == np.arange(SUBLANES)[None, None, :]).astype(np.float32)
    a_tail = jnp.einsum("qtd,hd->hqt", jnp.asarray(onehot), rel[:, :SUBLANES], precision=lax.Precision.HIGHEST)
    tmask_d = expand(a_tail, (tk[None, :] < n_new) & (tk[None, :] <= q8[:, None]))
    return hmask, hmask_last, tmask_m, tmask_d


def kernel(x_prompt, x_sample, cache_ckv, cache_kpe, cache_kb, cache_vb, page_table, meta_tokens, rel_bias,
           g_attn, w_in, g_cq, w_uq, g_qn, g_qr, g_ckv, g_kr, w_ukv, g_kn, g_qb, g_kb,
           lam_q1, lam_k1, lam_q2, lam_k2, g_sub, w_oa, w_ob, w_o, g_ffn, w_gu, w_down):
    assert w_in.shape[0] == 1, "single-layer trunk"
    nb, seq, d = x_prompt.shape
    nseq, n_new, _ = x_sample.shape
    pool, page = cache_ckv.shape[1], cache_ckv.shape[2]
    past_len = page_table.shape[1] * page
    rel_bias = rel_bias.astype(F32)

    fw, w_uk, w_uv = _prep_weights(g_attn[0], w_in[0], g_cq[0], w_uq[0], g_qn[0], g_qr[0], g_ckv[0],
                                   g_kr[0], w_ukv[0], g_kn[0], g_qb[0], g_kb[0])
    lamv = jnp.stack([lam_q1[0], lam_k1[0], lam_q2[0], lam_k2[0]]).astype(F32)
    g_sub_r = g_sub[0].reshape(1, -1).astype(F32)
    w_oa_b, w_ob_b, w_o_b = w_oa[0].astype(BF16), w_ob[0].astype(BF16), w_o[0].astype(BF16)
    g_ffn_r = g_ffn[0].reshape(1, -1).astype(F32)
    w_gu_b, w_down_b = w_gu[0].astype(BF16), w_down[0].astype(BF16)

    def trunk_tail(x2, oa2, ob2, gate2):
        x1 = _merge(x2, oa2, ob2, gate2, w_oa_b, w_ob_b, w_o_b)
        return _ffn(x1, g_ffn_r, w_gu_b, w_down_b)

    x_meta = jnp.pad(meta_tokens.astype(F32), ((0, ROW_TILE - N_META), (0, 0)))[None]
    ckv_m, kpe_m, kb_m, vb_m, _, _, k_m, vt_m = _features(x_meta, _rope_tables(jnp.arange(ROW_TILE)), fw, BF16)
    tab_p = _rope_tables(N_META + jnp.arange(seq))
    ckv_p, kpe_p, kb_p, vb_p, gate_p, q_p, k_p, vt_p = _features(x_prompt, tab_p, fw, BF16)
    oa_p, ob_p = _prompt_attention(q_p, k_p, vt_p, k_m[0, :N_META], vt_m[0, :, :N_META],
                                   _prompt_masks(rel_bias, N_META), lamv, g_sub_r)
    rows_p = nb * seq
    y_p = trunk_tail(x_prompt.reshape(rows_p, d), oa_p.reshape(rows_p, -1), ob_p.reshape(rows_p, -1),
                     gate_p.reshape(rows_p, -1))
    y_prompt = y_p.reshape(nb, seq, d)

    rows_s = nseq * n_new
    pos_s = past_len + jnp.tile(jnp.arange(n_new), nseq)
    xs = x_sample.reshape(1, rows_s, d)
    ckv_s, kpe_s, kb_s, vb_s, gate_s, q_s, _, _ = _features(xs, _rope_tables(pos_s), fw, F32)
    qg, qr, qd, ones_m = _sample_operands(q_s[0], g_kn[0].astype(F32), nseq, n_new)
    kpe_s32 = kpe_s[0][:, MLA_NOPE:MLA_NOPE + MLA_ROPE]
    pad8 = lambda a: jnp.pad(a, ((0, 0), (0, SUBLANES - n_new), (0, 0)))
    ckv_t = pad8(ckv_s[0].reshape(nseq, n_new, -1))
    kpet_t = jnp.pad(jnp.transpose(kpe_s32.reshape(nseq, n_new, -1), (0, 2, 1)),
                     ((0, 0), (0, 0), (0, TAIL_KEYS_MLA - n_new)))
    kb_t = kb_s[0].reshape(nseq, n_new * DIFF_HEADS, 2 * DIFF_QK)
    vb_t = vb_s[0].reshape(nseq, n_new * DIFF_HEADS, DIFF_V)
    hmask, hmask_last, tmask_m, tmask_d = _sample_masks(rel_bias, past_len, n_new, page)
    oa_s, ob_s = _sample_attention(
        page_table, cache_ckv, jnp.swapaxes(cache_kpe, 2, 3),
        cache_kb.reshape(1, pool, page * DIFF_HEADS, 2 * DIFF_QK),
        cache_vb.reshape(1, pool, page * DIFF_HEADS, DIFF_V),
        (qg, qr, qd), (ones_m, w_uk.astype(BF16), w_uv.astype(BF16), hmask, hmask_last),
        (ckv_t, kpet_t, kb_t, vb_t), (tmask_m, tmask_d), lamv, g_sub_r)
    y_s = trunk_tail(x_sample.reshape(rows_s, d), oa_s[:, :n_new].reshape(rows_s, -1),
                     ob_s[:, :n_new].reshape(rows_s, -1), gate_s[0])
    y_sample = y_s.reshape(nseq, n_new, d)

    def with_meta(a_meta, a_x):
        m = jnp.broadcast_to(a_meta[:, :N_META], (nb, N_META, a_meta.shape[-1]))
        return jnp.concatenate([m, a_x], axis=1)

    rope_lanes = slice(MLA_NOPE, MLA_NOPE + MLA_ROPE)
    t_real = seq + N_META
    outs_p = (with_meta(ckv_m, ckv_p)[None], with_meta(kpe_m[..., rope_lanes], kpe_p[..., rope_lanes])[None],
              with_meta(kb_m, kb_p).reshape(nb, t_real, DIFF_HEADS, 2 * DIFF_QK)[None],
              with_meta(vb_m, vb_p).reshape(nb, t_real, DIFF_HEADS, DIFF_V)[None])
    outs_s = (ckv_s[0].reshape(nseq, n_new, -1)[None], kpe_s32.reshape(nseq, n_new, -1)[None],
              kb_s[0].reshape(nseq, n_new, DIFF_HEADS, 2 * DIFF_QK)[None],
              vb_s[0].reshape(nseq, n_new, DIFF_HEADS, DIFF_V)[None])
    return (y_prompt, y_sample) + outs_p + outs_s
```
